```python
import math
import jax, jax.numpy as jnp
from jax import lax
import numpy as np

D_MODEL = 1024
BATCH = 16
SEQ = 256
DEPTH = 4
DEC_BATCH = 8
DEC_SEQ = 1024
PAST_LEN = 256

GRID_W = 64
N_EVEN = (DEPTH + 1) // 2
N_ODD = DEPTH // 2
D_A = D_MODEL
HEADDIM = 64
N_HEADS_A = D_A // HEADDIM
N_GROUPS_A = 4
D_STATE = 128
CONV_A = 3
CHUNK = 128
XBC_A = D_A + 2 * N_GROUPS_A * D_STATE
D_B = D_MODEL
CONV_B = 3
D_C = 2 * D_MODEL
POOL_WINDOWS = (2, 4, 8, 16)
N_POOL = len(POOL_WINDOWS)
D_POOL_GRP = D_C // N_POOL
SPLIT_EVEN = (D_A, D_A + XBC_A, D_A + XBC_A + N_HEADS_A, D_A + XBC_A + 2 * N_HEADS_A,
              D_A + XBC_A + 2 * N_HEADS_A + D_B, D_A + XBC_A + 2 * N_HEADS_A + 2 * D_B,
              D_A + XBC_A + 2 * N_HEADS_A + 3 * D_B)
IN_EVEN = D_A + XBC_A + 2 * N_HEADS_A + 4 * D_B
IN_ODD = 2 * D_C
ALPHA = (2 * DEPTH) ** 0.25
BETA = (8 * DEPTH) ** -0.25
LN_EPS = 1e-5
RMS_EPS = 1e-5
POS_BASE = 10000.0

kernel_name = 'hybrid_ssd_shortconv_pool_flow_step'

f32 = jnp.float32


def layer_norm(x, g, b):
    xf = x.astype(f32)
    mu = jnp.mean(xf, axis=-1, keepdims=True)
    var = jnp.mean(jnp.square(xf - mu), axis=-1, keepdims=True)
    return ((xf - mu) * lax.rsqrt(var + LN_EPS) * g.astype(f32) + b.astype(f32)).astype(x.dtype)


def rms_norm(x, g):
    xf = x.astype(f32)
    return xf * lax.rsqrt(jnp.mean(jnp.square(xf), axis=-1, keepdims=True) + RMS_EPS) * g.astype(f32)


def dwconv_centred(x, w):
    K = w.shape[0]
    p = K // 2
    L = x.shape[1]
    xp = jnp.pad(x, ((0, 0), (p, p), (0, 0)))
    return sum(xp[:, k:k + L] * w[k] for k in range(K))


def pos_embed_2d(L, dim):
    rows = L // GRID_W
    t = jnp.arange(rows * GRID_W)
    r = (t // GRID_W).astype(f32)
    col = (t % GRID_W).astype(f32)
    nf = dim // 4
    omega = 1.0 / (POS_BASE ** (jnp.arange(nf, dtype=f32) / nf))

    def emb(pos):
        a = pos[:, None] * omega[None, :]
        return jnp.concatenate([jnp.sin(a), jnp.cos(a)], axis=-1)

    return jnp.concatenate([emb(r), emb(col)], axis=-1)


def ssd_chunked(x, dt, A, bm, cm, h0):
    b, L, H, P = x.shape
    G, N = bm.shape[2], bm.shape[3]
    R = H // G
    Q = CHUNK
    nc = L // Q
    x = x.reshape(b, nc, Q, G, R, P)
    dt = dt.reshape(b, nc, Q, G, R)
    bm = bm.reshape(b, nc, Q, G, N)
    cm = cm.reshape(b, nc, Q, G, N)
    a_cum = jnp.cumsum(dt * A.reshape(G, R), axis=2)
    lower = jnp.tril(jnp.ones((Q, Q), dtype=bool))[None, None, :, :, None, None]
    seg = a_cum[:, :, :, None] - a_cum[:, :, None, :]
    decay = jnp.exp(jnp.where(lower, seg, -jnp.inf))
    xdt = x * dt[..., None]
    cb = jnp.einsum('bcign,bcjgn->bcijg', cm, bm)
    y_diag = jnp.einsum('bcijgr,bcjgrp->bcigrp', cb[..., None] * decay, xdt)
    decay_end = jnp.exp(a_cum[:, :, -1:] - a_cum)
    chunk_states = jnp.einsum('bcjgn,bcjgrp->bcgrpn', bm, decay_end[..., None] * xdt)
    chunk_decay = jnp.exp(a_cum[:, :, -1])

    def step(h, inp):
        s, d = inp
        return h * d[..., None, None] + s, h

    h_last, h_starts = lax.scan(step, h0.astype(f32).reshape(b, G, R, P, N),
                                (jnp.moveaxis(chunk_states, 1, 0), jnp.moveaxis(chunk_decay, 1, 0)))
    h_starts = jnp.moveaxis(h_starts, 0, 1)
    y_off = jnp.einsum('bcign,bcgrpn->bcigrp', cm, h_starts) * jnp.exp(a_cum)[..., None]
    return (y_diag + y_off).reshape(b, L, H, P), h_last.reshape(b, H, P, N)


def ssd_mixer(z, xbc, dtf, dtb, conv_w, conv_b, a_log, dt_bias, d_skip, norm_g, h0_f, h0_b):
    Bsz, L, _ = z.shape
    xbc = jax.nn.silu(dwconv_centred(xbc, conv_w) + conv_b)
    xs, bm, cm = jnp.split(xbc, [D_A, D_A + N_GROUPS_A * D_STATE], axis=-1)
    xs = xs.reshape(Bsz, L, N_HEADS_A, HEADDIM).astype(f32)
    bm = bm.reshape(Bsz, L, N_GROUPS_A, D_STATE).astype(f32)
    cm = cm.reshape(Bsz, L, N_GROUPS_A, D_STATE).astype(f32)
    A = -jnp.exp(a_log.astype(f32))
    dt = jax.nn.softplus(jnp.stack([dtf, dtb], 0).astype(f32) + dt_bias.astype(f32)[:, None, None, :])
    y_f, h_f = ssd_chunked(xs, dt[0], A[0], bm, cm, h0_f)
    y_b, h_b = ssd_chunked(xs[:, ::-1], dt[1][:, ::-1], A[1], bm[:, ::-1], cm[:, ::-1], h0_b)
    y = y_f + y_b[:, ::-1] + d_skip.astype(f32)[:, None] * xs
    y = y.reshape(Bsz, L, D_A) * jax.nn.silu(z.astype(f32))
    return rms_norm(y, norm_g).astype(z.dtype), h_f, h_b


def multiscale_pool(v):
    L = v.shape[1]
    vf = v.astype(f32)
    cs = jnp.pad(jnp.cumsum(vf, axis=1), ((0, 0), (1, 0), (0, 0)))
    t = jnp.arange(L)
    outs = []
    for k, w in enumerate(POOL_WINDOWS):
        lo = jnp.clip(t - w // 2, 0, L)
        hi = jnp.clip(t + w - w // 2, 0, L)
        sl = slice(k * D_POOL_GRP, (k + 1) * D_POOL_GRP)
        seg = cs[:, :, sl]
        cnt = (hi - lo).astype(f32)[None, :, None]
        outs.append((seg[:, hi] - seg[:, lo]) / cnt - vf[:, :, sl])
    return jnp.concatenate(outs, axis=-1).astype(v.dtype)


def trunk(x, cvec, h0, w_mod, b_mod, ln_g, ln_b, w_in_even, conv_a_w, conv_a_b, a_log, dt_bias,
          d_skip, norm_a_g, conv_b_w, w_out_even, w_in_odd, w_pool, pool_scale, w_out_odd):
    Bsz, L, _ = x.shape
    states = []
    for i in range(DEPTH):
        j = i // 2
        mod = jnp.dot(jax.nn.silu(cvec), w_mod[i]) + b_mod[i]
        shift, scale, gate = jnp.split(mod[:, None, :], 3, axis=-1)
        u = x * (1 + scale) + shift
        if i % 2 == 0:
            proj = u @ w_in_even[j]
            z, xbc, dtf, dtb, g, bg, cg, h_in = jnp.split(proj, SPLIT_EVEN, axis=-1)
            y_a, h_f, h_b = ssd_mixer(z, xbc, dtf, dtb, conv_a_w[j], conv_a_b[j], a_log[j], dt_bias[j],
                                      d_skip[j], norm_a_g[j], h0[:, j, 0], h0[:, j, 1])
            y_b = bg * dwconv_centred(cg * h_in, conv_b_w[j]) * jax.nn.silu(g)
            y = jnp.concatenate([y_a, y_b], axis=-1) @ w_out_even[j]
            states.append(jnp.stack([h_f, h_b], axis=1).astype(h0.dtype))
        else:
            v, g = jnp.split(u @ w_in_odd[j], 2, axis=-1)
            p = multiscale_pool(v).reshape(Bsz, L, N_POOL, D_POOL_GRP)
            p = jnp.einsum('blkc,kcd->blkd', p, w_pool[j]).reshape(Bsz, L, D_C) * pool_scale[j]
            y = (p * jax.nn.silu(g)) @ w_out_odd[j]
        x = layer_norm(ALPHA * x + (1 + gate) * y, ln_g[i], ln_b[i])
    return x, jnp.stack(states, axis=1)


def setup_inputs(seed: int = 0) -> dict:
    key = jax.random.key(seed)
    ks = jax.random.split(key, 24)
    D = D_MODEL
    nrm = jax.random.normal
    dt0 = jnp.exp(jax.random.uniform(ks[11], (N_EVEN, 2, N_HEADS_A), minval=math.log(1e-3), maxval=math.log(1e-1)))
    return {
        'x_prompt': nrm(ks[0], (BATCH, SEQ, D), f32),
        'x_sample': nrm(ks[1], (DEC_BATCH, DEC_SEQ, D), f32),
        'state_ssd': 0.1 * nrm(ks[2], (DEC_BATCH, N_EVEN, 2, N_HEADS_A, HEADDIM, D_STATE), f32),
        'c': nrm(ks[3], (DEC_BATCH, D), f32),
        'c_ctx': nrm(ks[4], (D,), f32),
        'w_mod': 0.5 * D ** -0.5 * nrm(ks[5], (DEPTH, D, 3 * D), f32),
        'b_mod': 0.01 * nrm(ks[6], (DEPTH, 3 * D), f32),
        'ln_g': 1.0 + 0.02 * nrm(ks[7], (DEPTH, D), f32),
        'ln_b': 0.02 * nrm(ks[8], (DEPTH, D), f32),
        'w_in_even': D ** -0.5 * nrm(ks[9], (N_EVEN, D, IN_EVEN), f32),
        'conv_a_w': CONV_A ** -0.5 * nrm(ks[10], (N_EVEN, CONV_A, XBC_A), f32),
        'conv_a_b': 0.02 * nrm(ks[12], (N_EVEN, XBC_A), f32),
        'a_log': jnp.log(jax.random.uniform(ks[13], (N_EVEN, 2, N_HEADS_A), minval=1.0, maxval=16.0)),
        'dt_bias': dt0 + jnp.log(-jnp.expm1(-dt0)),
        'd_skip': 1.0 + 0.1 * nrm(ks[14], (N_EVEN, N_HEADS_A), f32),
        'norm_a_g': 1.0 + 0.02 * nrm(ks[15], (N_EVEN, D_A), f32),
        'conv_b_w': CONV_B ** -0.5 * nrm(ks[16], (N_EVEN, CONV_B, D_B), f32),
        'w_out_even': BETA * (D_A + D_B) ** -0.5 * nrm(ks[17], (N_EVEN, D_A + D_B, D), f32),
        'w_in_odd': D ** -0.5 * nrm(ks[18], (N_ODD, D, IN_ODD), f32),
        'w_pool': D_POOL_GRP ** -0.5 * nrm(ks[19], (N_ODD, N_POOL, D_POOL_GRP, D_POOL_GRP), f32),
        'pool_scale': 1.0 + 0.1 * nrm(ks[20], (N_ODD, D_C), f32),
        'w_out_odd': BETA * D_C ** -0.5 * nrm(ks[21], (N_ODD, D_C, D), f32),
    }


def reference(x_prompt, x_sample, state_ssd, c, c_ctx, w_mod, b_mod, ln_g, ln_b, w_in_even, conv_a_w,
              conv_a_b, a_log, dt_bias, d_skip, norm_a_g, conv_b_w, w_out_even, w_in_odd, w_pool,
              pool_scale, w_out_odd):
    h0_ctx = jnp.zeros((x_prompt.shape[0], N_EVEN, 2, N_HEADS_A, HEADDIM, D_STATE), dtype=x_prompt.dtype)
    y_prompt, new_state_ssd = trunk(x_prompt, c_ctx[None, :], h0_ctx, w_mod, b_mod, ln_g, ln_b, w_in_even,
                                    conv_a_w, conv_a_b, a_log, dt_bias, d_skip, norm_a_g, conv_b_w,
                                    w_out_even, w_in_odd, w_pool, pool_scale, w_out_odd)
    L = x_sample.shape[1]
    xs = x_sample + pos_embed_2d(L, x_sample.shape[2]).astype(x_sample.dtype)[None]
    y_sample, _ = trunk(xs, c, state_ssd, w_mod, b_mod, ln_g, ln_b, w_in_even, conv_a_w, conv_a_b,
                        a_log, dt_bias, d_skip, norm_a_g, conv_b_w, w_out_even, w_in_odd, w_pool,
                        pool_scale, w_out_odd)
    return (y_prompt, y_sample, new_state_ssd)
```

```python
import functools
import math

import jax
import jax.numpy as jnp
from jax import lax
from jax.experimental import pallas as pl
from jax.experimental.pallas import tpu as pltpu

f32 = jnp.float32
bf16 = jnp.bfloat16

D = 1024
DEPTH = 4
GRID_W = 64
N_HEADS = 16
HEADDIM = 64
N_GROUPS = 4
D_STATE = 128
Q = 128
XBC = D + 2 * N_GROUPS * D_STATE
D_C = 2 * D
POOL_WINDOWS = (2, 4, 8, 16)
D_POOL_GRP = D_C // len(POOL_WINDOWS)
ALPHA = (2 * DEPTH) ** 0.25
LN_EPS = 1e-5
RMS_EPS = 1e-5
POS_BASE = 10000.0

LANES = 128
SUBLANES = 8
TM = 256
MAIN_COLS = 7 * D
VMEM_LIMIT = 56 * 1024 * 1024

_ARB = "arbitrary"


def _dot(a, b):
    return jnp.dot(a, b, preferred_element_type=f32)


def _silu(x):
    return x * (1.0 / (1.0 + jnp.exp(-x)))


def _softplus(x):
    return jnp.maximum(x, 0.0) + jnp.log1p(jnp.exp(-jnp.abs(x)))


def _layer_norm(r, g, b):
    mu = jnp.mean(r, axis=-1, keepdims=True)
    d = r - mu
    var = jnp.mean(d * d, axis=-1, keepdims=True)
    return d * lax.rsqrt(var + LN_EPS) * g + b


def _const_spec(shape):
    nd = len(shape)
    return pl.BlockSpec(shape, lambda *_: (0,) * nd)


def _halo_rows(xp_ref, xn_ref, scale, shift, first, last):
    up = xp_ref[...] * (1.0 + scale) + shift
    un = xn_ref[...] * (1.0 + scale) + shift
    up = jnp.where(first, 0.0, up)
    un = jnp.where(last, 0.0, un)
    return jnp.concatenate([up, un], axis=0).astype(bf16)


def _conv3(p, p_prev, p_next, w):
    n = p.shape[0]
    row = lax.broadcasted_iota(jnp.int32, p.shape, 0)
    pm = jnp.where(row == 0, p_prev, pltpu.roll(p, 1, 0))
    pp = jnp.where(row == n - 1, p_next, pltpu.roll(p, n - 1, 0))
    return pm * w[0:1, :] + p * w[1:2, :] + pp * w[2:3, :]


def _mod_kernel(c_ref, w_ref, b_ref, o_ref):
    c = _silu(c_ref[...])
    o_ref[...] = jnp.dot(c, w_ref[...], preferred_element_type=f32,
                         precision=lax.Precision.HIGHEST) + b_ref[...]


def _modulation(cvec, w_mod, b_mod):
    rows = cvec.shape[0]
    tn = 1024
    nt = 3 * D // tn
    return pl.pallas_call(
        _mod_kernel,
        grid=(DEPTH, nt),
        in_specs=[
            pl.BlockSpec((rows, D), lambda i, n: (0, 0)),
            pl.BlockSpec((None, D, tn), lambda i, n: (i, 0, n)),
            pl.BlockSpec((None, 1, tn), lambda i, n: (i, 0, n)),
        ],
        out_specs=pl.BlockSpec((None, rows, tn), lambda i, n: (i, 0, n)),
        out_shape=jax.ShapeDtypeStruct((DEPTH, rows, 3 * D), f32),
        compiler_params=pltpu.CompilerParams(dimension_semantics=(_ARB, _ARB)),
        name="modulation",
    )(cvec, w_mod, b_mod.reshape(DEPTH, 1, 3 * D))


def _pos_embed_2d(L, dim):
    rows = L // GRID_W
    t = jnp.arange(rows * GRID_W)
    r = (t // GRID_W).astype(f32)
    col = (t % GRID_W).astype(f32)
    nf = dim // 4
    omega = 1.0 / (POS_BASE ** (jnp.arange(nf, dtype=f32) / nf))

    def emb(pos):
        a = pos[:, None] * omega[None, :]
        return jnp.concatenate([jnp.sin(a), jnp.cos(a)], axis=-1)

    return jnp.concatenate([emb(r), emb(col)], axis=-1)


def _add_pos_kernel(x_ref, p_ref, o_ref):
    o_ref[...] = x_ref[...] + p_ref[...]


def _add_pos(x, pos):
    b, L, d = x.shape
    return pl.pallas_call(
        _add_pos_kernel,
        grid=(b, L // TM),
        in_specs=[pl.BlockSpec((None, TM, d), lambda s, t: (s, t, 0)),
                  pl.BlockSpec((TM, d), lambda s, t: (t, 0))],
        out_specs=pl.BlockSpec((None, TM, d), lambda s, t: (s, t, 0)),
        out_shape=jax.ShapeDtypeStruct((b, L, d), f32),
        compiler_params=pltpu.CompilerParams(dimension_semantics=(_ARB, _ARB)),
        name="add_pos",
    )(x, pos)


def _even_proj_kernel(x_ref, xp_ref, xn_ref, sc_ref, sh_ref, wm_ref, wdt_ref, caw_ref, cab_ref,
                      cbw_ref, z_ref, xbc_ref, yb_ref, dtr_ref, *, tps):
    t = pl.program_id(0)
    first = (t % tps) == 0
    last = (t % tps) == tps - 1
    scale = sc_ref[...]
    shift = sh_ref[...]
    u = (x_ref[...] * (1.0 + scale) + shift).astype(bf16)
    uh = _halo_rows(xp_ref, xn_ref, scale, shift, first, last)

    z_ref[...] = _dot(u, wm_ref[:, 0:D])
    dtr_ref[...] = _dot(u, wdt_ref[...])

    cw = 512
    for cb in range(XBC // cw):
        wcols = slice(D + cb * cw, D + (cb + 1) * cw)
        cols = slice(cb * cw, (cb + 1) * cw)
        p = _dot(u, wm_ref[:, wcols])
        ph = _dot(uh, wm_ref[:, wcols])
        cv = _conv3(p, ph[SUBLANES - 1:SUBLANES, :], ph[SUBLANES:SUBLANES + 1, :], caw_ref[:, cols])
        xbc_ref[:, cols] = _silu(cv + cab_ref[:, cols]).astype(bf16)

    for cb in range(D // cw):
        cols = slice(cb * cw, (cb + 1) * cw)

        def wsl(k):
            return slice((3 + k) * D + cb * cw, (3 + k) * D + (cb + 1) * cw)

        g = _dot(u, wm_ref[:, wsl(0)])
        bg = _dot(u, wm_ref[:, wsl(1)])
        ch = _dot(u, wm_ref[:, wsl(2)]) * _dot(u, wm_ref[:, wsl(3)])
        chh = _dot(uh, wm_ref[:, wsl(2)]) * _dot(uh, wm_ref[:, wsl(3)])
        cv = _conv3(ch, chh[SUBLANES - 1:SUBLANES, :], chh[SUBLANES:SUBLANES + 1, :], cbw_ref[:, cols])
        yb_ref[:, cols] = (bg * cv * _silu(g)).astype(bf16)


def _halo_specs(n_tiles):
    bpt = TM // SUBLANES
    nblk = n_tiles * bpt
    prev = pl.BlockSpec((SUBLANES, D), lambda t: (jnp.maximum(t * bpt - 1, 0), 0))
    nxt = pl.BlockSpec((SUBLANES, D), lambda t: (jnp.minimum((t + 1) * bpt, nblk - 1), 0))
    return prev, nxt


def _even_proj(x, mods4, layer, mod_row, tps, wm, wdt, caw, cab, cbw):
    T = x.shape[0]
    n_tiles = T // TM
    prev, nxt = _halo_specs(n_tiles)

    def mod_spec(part):
        return pl.BlockSpec((None, None, 1, D), lambda t: (layer, mod_row(t), 0, part))

    row = lambda w: pl.BlockSpec((TM, w), lambda t: (t, 0))
    return pl.pallas_call(
        functools.partial(_even_proj_kernel, tps=tps),
        grid=(n_tiles,),
        in_specs=[row(D), prev, nxt, mod_spec(1), mod_spec(0),
                  _const_spec(wm.shape), _const_spec(wdt.shape), _const_spec(caw.shape),
                  _const_spec(cab.shape), _const_spec(cbw.shape)],
        out_specs=[row(D), row(XBC), row(D), row(LANES)],
        out_shape=[jax.ShapeDtypeStruct((T, D), f32),
                   jax.ShapeDtypeStruct((T, XBC), bf16),
                   jax.ShapeDtypeStruct((T, D), bf16),
                   jax.ShapeDtypeStruct((T, LANES), f32)],
        compiler_params=pltpu.CompilerParams(dimension_semantics=(_ARB,),
                                             vmem_limit_bytes=VMEM_LIMIT),
        name="even_proj",
    )(x, x, x, mods4, mods4, wm, wdt, caw, cab, cbw)


def _split3(x):
    h = x.astype(bf16)
    r = x - h.astype(f32)
    m = r.astype(bf16)
    l = (r - m.astype(f32)).astype(bf16)
    return h, m, l


def _expand(m, e_ref):
    hi = m.astype(bf16)
    lo = (m - hi.astype(f32)).astype(bf16)
    return _dot(jnp.concatenate([hi, lo], axis=1), e_ref[...])


def _even_scan_kernel(*refs, nc, has_h0, want_h):
    it = iter(refs)
    xbc_ref, dtr_ref, z_ref, yb_ref, x_ref, gate_ref = (next(it) for _ in range(6))
    h0_ref = next(it) if has_h0 else None
    (alog_ref, dtb_ref, dsk_ref, ng_ref, lng_ref, lnb_ref, ef_ref, eb_ref, tri_ref,
     wout_ref) = (next(it) for _ in range(10))
    xo_ref = next(it)
    hl_ref = next(it) if want_h else None
    yacc, w_s, e_s, dec_s, hf, hb = (next(it) for _ in range(6))

    k = pl.program_id(1)
    is_fwd = k < nc
    c = jnp.where(is_fwd, k, 2 * nc - 1 - k)
    rows = pl.ds(pl.multiple_of(c * Q, Q), Q)
    drows = pl.ds(pl.multiple_of(c * SUBLANES, SUBLANES), SUBLANES)

    @pl.when(k == 0)
    def _():
        if has_h0:
            hf[...] = h0_ref[0].T
            hb[...] = h0_ref[1].T
        else:
            hf[...] = jnp.zeros_like(hf)
            hb[...] = jnp.zeros_like(hb)

    def load_xbc():
        xbc = xbc_ref[...]
        xs = xbc[:, 0:D]
        return xs, xs.astype(f32), xbc[:, D:D + N_GROUPS * D_STATE], xbc[:, D + N_GROUPS * D_STATE:XBC]

    def group_bt(bm, g):
        return bm[:, g * D_STATE:(g + 1) * D_STATE].astype(f32).T.astype(bf16)

    @pl.when(is_fwd)
    def _():
        lane = lax.broadcasted_iota(jnp.int32, (Q, LANES), 1)
        is_f = lane < N_HEADS
        is_b = jnp.logical_and(lane >= N_HEADS, lane < 2 * N_HEADS)
        a_row = jnp.where(lane[0:1] < 2 * N_HEADS, -jnp.exp(alog_ref[...]), 0.0)
        dt = _softplus(dtr_ref[...] + dtb_ref[...])
        a = dt * a_row
        stacked = jnp.concatenate([jnp.where(is_f, a, 0.0), jnp.where(is_b, a, 0.0)], axis=0)
        tri = tri_ref[...]
        cum = sum(_dot(tri, part) for part in _split3(stacked))
        tot = jnp.where(is_f[0:1], cum[Q - 1:Q, :], cum[0:1, :])
        w = dt * jnp.exp(tot - cum)
        e = jnp.exp(cum)
        dec = jnp.broadcast_to(jnp.exp(tot), (SUBLANES, LANES))
        w_s[rows, :] = w
        e_s[rows, :] = e
        dec_s[drows, :] = dec
        cum_t = cum.T
        dt_t = dt.T
        wx = _expand(w, ef_ref)
        ex = _expand(e, ef_ref)
        decx = _expand(dec, ef_ref)[0:1, :]

        xs, xf, bm, cm = load_xbc()
        ri = lax.broadcasted_iota(jnp.int32, (Q, Q), 0)
        ci = lax.broadcasted_iota(jnp.int32, (Q, Q), 1)
        lower = ri >= ci
        upper = ri <= ci
        lane_q = lax.broadcasted_iota(jnp.int32, (Q, LANES), 1)
        lo_half = lane_q < HEADDIM

        def build_m(cbm, h):
            cf = jnp.broadcast_to(cum[:, h:h + 1], (Q, Q))
            cbk = jnp.broadcast_to(cum[:, N_HEADS + h:N_HEADS + h + 1], (Q, Q))
            arg = jnp.where(lower, cf - cum_t[h:h + 1, :], cbk - cum_t[N_HEADS + h:N_HEADS + h + 1, :])
            wg = (jnp.where(lower, dt_t[h:h + 1, :], 0.0)
                  + jnp.where(upper, dt_t[N_HEADS + h:N_HEADS + h + 1, :], 0.0))
            return (cbm * jnp.exp(arg) * wg).astype(bf16)

        for g in range(N_GROUPS):
            gc = slice(g * 256, (g + 1) * 256)
            btg = group_bt(bm, g)
            cg = cm[:, g * D_STATE:(g + 1) * D_STATE]
            cbm = _dot(cg, btg)
            hg = hf[:, gc]
            yoff = _dot(cg, hg.astype(bf16)) * ex[:, gc]
            for pr in range(2):
                kp = 2 * g + pr
                pc = slice(kp * LANES, (kp + 1) * LANES)
                m01 = jnp.concatenate([build_m(cbm, 2 * kp), build_m(cbm, 2 * kp + 1)], axis=1)
                xp = xf[:, pc]
                rhs = jnp.concatenate([jnp.where(lo_half, xp, 0.0), jnp.where(lo_half, 0.0, xp)],
                                      axis=0).astype(bf16)
                y = _dot(m01, rhs) + yoff[:, pr * LANES:(pr + 1) * LANES] + dsk_ref[:, pc] * xp
                yacc[rows, pc] = y
            xw = (xf[:, gc] * wx[:, gc]).astype(bf16)
            hf[:, gc] = hg * decx[:, gc] + _dot(btg, xw)

        if want_h:
            @pl.when(k == nc - 1)
            def _():
                hl_ref[0] = hf[...].T

    @pl.when(jnp.logical_not(is_fwd))
    def _():
        w = w_s[rows, :]
        e = e_s[rows, :]
        dec = dec_s[drows, :]
        wx = _expand(w, eb_ref)
        ex = _expand(e, eb_ref)
        decx = _expand(dec, eb_ref)[0:1, :]
        xs, xf, bm, cm = load_xbc()
        ys = []
        for g in range(N_GROUPS):
            gc = slice(g * 256, (g + 1) * 256)
            btg = group_bt(bm, g)
            cg = cm[:, g * D_STATE:(g + 1) * D_STATE]
            hg = hb[:, gc]
            ys.append(yacc[rows, gc] + _dot(cg, hg.astype(bf16)) * ex[:, gc])
            xw = (xf[:, gc] * wx[:, gc]).astype(bf16)
            hb[:, gc] = hg * decx[:, gc] + _dot(btg, xw)
        y = jnp.concatenate(ys, axis=1) * _silu(z_ref[...])
        ms = jnp.mean(y * y, axis=-1, keepdims=True)
        ya = (y * lax.rsqrt(ms + RMS_EPS) * ng_ref[...]).astype(bf16)
        o = _dot(jnp.concatenate([ya, yb_ref[...]], axis=1), wout_ref[...])
        r = ALPHA * x_ref[...] + (1.0 + gate_ref[...]) * o
        xo_ref[...] = _layer_norm(r, lng_ref[...], lnb_ref[...])

        if want_h:
            @pl.when(k == 2 * nc - 1)
            def _():
                hl_ref[1] = hb[...].T


def _even_scan(xbc, dtr, z, yb, x, mods4, layer, mod_row, h0, j, consts, nseq, L, want_h):
    nc = L // Q
    has_h0 = h0 is not None

    def cidx(k):
        return jnp.where(k < nc, k, 2 * nc - 1 - k)

    def bidx(k):
        return jnp.where(k < nc, nc - 1, 2 * nc - 1 - k)

    def any_chunk(w):
        return pl.BlockSpec((Q, w), lambda s, k: (s * nc + cidx(k), 0))

    def back_chunk(w):
        return pl.BlockSpec((Q, w), lambda s, k: (s * nc + bidx(k), 0))

    in_specs = [any_chunk(XBC), any_chunk(LANES), back_chunk(D), back_chunk(D), back_chunk(D),
                pl.BlockSpec((None, None, 1, D), lambda s, k: (layer, mod_row(s), 0, 2))]
    args = [xbc, dtr, z, yb, x, mods4]
    if has_h0:
        in_specs.append(pl.BlockSpec((None, None, 2, D, D_STATE), lambda s, k: (s, j, 0, 0, 0)))
        args.append(h0)
    in_specs += [_const_spec(a.shape) for a in consts]
    args += list(consts)

    out_specs = [back_chunk(D)]
    out_shape = [jax.ShapeDtypeStruct((nseq * L, D), f32)]
    if want_h:
        out_specs.append(pl.BlockSpec((None, 2, D, D_STATE), lambda s, k: (s, 0, 0, 0)))
        out_shape.append(jax.ShapeDtypeStruct((nseq, 2, D, D_STATE), f32))

    res = pl.pallas_call(
        functools.partial(_even_scan_kernel, nc=nc, has_h0=has_h0, want_h=want_h),
        grid=(nseq, 2 * nc),
        in_specs=in_specs,
        out_specs=out_specs,
        out_shape=out_shape,
        scratch_shapes=[pltpu.VMEM((L, D), f32), pltpu.VMEM((L, LANES), f32),
                        pltpu.VMEM((L, LANES), f32), pltpu.VMEM((nc * SUBLANES, LANES), f32),
                        pltpu.VMEM((D_STATE, D), f32), pltpu.VMEM((D_STATE, D), f32)],
        compiler_params=pltpu.CompilerParams(dimension_semantics=(_ARB, _ARB),
                                             vmem_limit_bytes=VMEM_LIMIT),
        name="even_scan",
    )(*args)
    return res if want_h else (res[0], None)


def _odd_kernel(x_ref, xp_ref, xn_ref, sc_ref, sh_ref, gt_ref, win_ref, wp_ref, ps_ref, wout_ref,
                lng_ref, lnb_ref, xo_ref, ycat, *, tps, L):
    t = pl.program_id(0)
    first = (t % tps) == 0
    last = (t % tps) == tps - 1
    scale = sc_ref[...]
    shift = sh_ref[...]
    x = x_ref[...]
    u = (x * (1.0 + scale) + shift).astype(bf16)
    uh = _halo_rows(xp_ref, xn_ref, scale, shift, first, last)

    gw = D_POOL_GRP
    n_ext = TM + 2 * SUBLANES
    pos = (t % tps) * TM + lax.broadcasted_iota(jnp.int32, (TM, gw), 0)
    for kk, win in enumerate(POOL_WINDOWS):
        cols = slice(kk * gw, (kk + 1) * gw)
        v = _dot(u, win_ref[:, cols])
        vh = _dot(uh, win_ref[:, cols])
        s = jnp.concatenate([vh[0:SUBLANES], v, vh[SUBLANES:2 * SUBLANES]], axis=0)
        s = pltpu.roll(s, 1, 0) + s
        half = 1
        while 2 * half < win:
            s = pltpu.roll(s, half, 0) + pltpu.roll(s, n_ext - half, 0)
            half *= 2
        wsum = s[SUBLANES:SUBLANES + TM, :]
        lo = jnp.maximum(pos - win // 2, 0)
        hi = jnp.minimum(pos + win - win // 2, L)
        p = wsum / (hi - lo).astype(f32) - v
        q = _dot(p.astype(bf16), wp_ref[kk]) * ps_ref[:, cols]
        g = _dot(u, win_ref[:, D_C + kk * gw:D_C + (kk + 1) * gw])
        ycat[:, cols] = (q * _silu(g)).astype(bf16)

    o = _dot(ycat[...], wout_ref[...])
    r = ALPHA * x + (1.0 + gt_ref[...]) * o
    xo_ref[...] = _layer_norm(r, lng_ref[...], lnb_ref[...])


def _odd_layer(x, mods4, layer, mod_row, tps, L, win, wp, ps, wout, lng, lnb):
    T = x.shape[0]
    n_tiles = T // TM
    prev, nxt = _halo_specs(n_tiles)

    def mod_spec(part):
        return pl.BlockSpec((None, None, 1, D), lambda t: (layer, mod_row(t), 0, part))

    row = pl.BlockSpec((TM, D), lambda t: (t, 0))
    return pl.pallas_call(
        functools.partial(_odd_kernel, tps=tps, L=L),
        grid=(n_tiles,),
        in_specs=[row, prev, nxt, mod_spec(1), mod_spec(0), mod_spec(2),
                  _const_spec(win.shape), _const_spec(wp.shape), _const_spec(ps.shape),
                  _const_spec(wout.shape), _const_spec(lng.shape), _const_spec(lnb.shape)],
        out_specs=row,
        out_shape=jax.ShapeDtypeStruct((T, D), f32),
        scratch_shapes=[pltpu.VMEM((TM, D_C), bf16)],
        compiler_params=pltpu.CompilerParams(dimension_semantics=(_ARB,),
                                             vmem_limit_bytes=VMEM_LIMIT),
        name="odd_layer",
    )(x, x, x, mods4, mods4, mods4, win, wp, ps, wout, lng, lnb)


def _scan_constants():
    l = jnp.arange(2 * LANES)[:, None] % LANES
    ch = jnp.arange(D)[None, :] // HEADDIM
    ef = (l == ch).astype(bf16)
    eb = (l == ch + N_HEADS).astype(bf16)
    i = jnp.arange(Q)[:, None]
    jj = jnp.arange(Q)[None, :]
    tri = jnp.concatenate([(i >= jj), (i <= jj)], axis=1).astype(bf16)
    return ef, eb, tri


def _pad_lanes(v):
    return jnp.pad(v.reshape(1, -1), ((0, 0), (0, LANES - v.size)))


def kernel(x_prompt, x_sample, state_ssd, c, c_ctx, w_mod, b_mod, ln_g, ln_b, w_in_even, conv_a_w,
           conv_a_b, a_log, dt_bias, d_skip, norm_a_g, conv_b_w, w_out_even, w_in_odd, w_pool,
           pool_scale, w_out_odd):
    nb, sl, _ = x_prompt.shape
    db, dl, _ = x_sample.shape
    n_even = w_in_even.shape[0]

    n_rows = 1 + db
    pad_rows = -n_rows % SUBLANES
    cvec = jnp.concatenate([c_ctx[None, :], c, jnp.zeros((pad_rows, D), f32)], axis=0)
    mods = _modulation(cvec, w_mod, b_mod)
    mods4 = mods.reshape(DEPTH, n_rows + pad_rows, 1, 3 * D)

    x_ctx = x_prompt.reshape(nb * sl, D)
    x_lat = _add_pos(x_sample, _pos_embed_2d(dl, D)).reshape(db * dl, D)
    h0_all = state_ssd.reshape(db, n_even, 2, D, D_STATE)

    ef, eb, tri = _scan_constants()
    tps_ctx, tps_lat = sl // TM, dl // TM
    ctx_row_t = lambda t: 0
    lat_row_t = lambda t: 1 + t // tps_lat
    ctx_row_s = lambda s: 0
    lat_row_s = lambda s: 1 + s

    states = []
    for i in range(DEPTH):
        j = i // 2
        lng = ln_g[i].reshape(1, D)
        lnb = ln_b[i].reshape(1, D)
        if i % 2 == 0:
            w = w_in_even[j]
            wm = jnp.concatenate([w[:, :D + XBC], w[:, D + XBC + 2 * N_HEADS:]], axis=1).astype(bf16)
            wdt = jnp.pad(w[:, D + XBC:D + XBC + 2 * N_HEADS],
                          ((0, 0), (0, LANES - 2 * N_HEADS))).astype(bf16)
            caw = conv_a_w[j]
            cab = conv_a_b[j].reshape(1, XBC)
            cbw = conv_b_w[j]
            consts = (_pad_lanes(a_log[j]), _pad_lanes(dt_bias[j]),
                      jnp.repeat(d_skip[j], HEADDIM).reshape(1, D), norm_a_g[j].reshape(1, D),
                      lng, lnb, ef, eb, tri, w_out_even[j].astype(bf16))
            z, xbc, yb, dtr = _even_proj(x_ctx, mods4, i, ctx_row_t, tps_ctx, wm, wdt, caw, cab, cbw)
            x_ctx, h_last = _even_scan(xbc, dtr, z, yb, x_ctx, mods4, i, ctx_row_s, None, j, consts,
                                       nb, sl, True)
            states.append(h_last)
            z, xbc, yb, dtr = _even_proj(x_lat, mods4, i, lat_row_t, tps_lat, wm, wdt, caw, cab, cbw)
            x_lat, _ = _even_scan(xbc, dtr, z, yb, x_lat, mods4, i, lat_row_s, h0_all, j, consts,
                                  db, dl, False)
        else:
            win = w_in_odd[j].astype(bf16)
            wp = w_pool[j].astype(bf16)
            ps = pool_scale[j].reshape(1, D_C)
            wout = w_out_odd[j].astype(bf16)
            x_ctx = _odd_layer(x_ctx, mods4, i, ctx_row_t, tps_ctx, sl, win, wp, ps, wout, lng, lnb)
            x_lat = _odd_layer(x_lat, mods4, i, lat_row_t, tps_lat, dl, win, wp, ps, wout, lng, lnb)

    new_state = jnp.stack(states, axis=1).reshape(nb, n_even, 2, N_HEADS, HEADDIM, D_STATE)
    return (x_ctx.reshape(nb, sl, D), x_lat.reshape(db, dl, D), new_state)
```
